```python
import math
import jax
import jax.numpy as jnp
from jax import lax
import numpy as np

D_MODEL = 2048
BATCH = 32
SEQ = 256
DEPTH = 2
DEC_BATCH = 8
DEC_SEQ = 2048
PAST_LEN = 256

GRID_W = 64
HG_DIM = 128
HG_WIDTH = D_MODEL // 4
HG_HEADS = HG_WIDTH // HG_DIM
S5_GROUP = 16
S5_STATE = 64
S5_WIDTH = D_MODEL // 4
S5_GROUPS = S5_WIDTH // S5_GROUP
NA_DIM = 64
NA_WIDTH = D_MODEL // 2
NA_HEADS = NA_WIDTH // NA_DIM
WIN_ROWS = 8
WIN_COLS = 16
MIX_WIDTH = HG_WIDTH + S5_WIDTH + NA_WIDTH
IN_WIDTH = 5 * HG_WIDTH + S5_WIDTH + 3 * NA_WIDTH
CHUNK = 64
N_EXPERTS = 32
TOP_K = 4
D_FF = D_MODEL
SWIGLU_LIMIT = 7.0
SWIGLU_ALPHA = 1.702
EXPERT_BLOCK = 256
EPS = 1e-6

kernel_name = 'hybrid_dit_hgrn2_s5_natten_moe_step'


def rmsnorm(x, gain):
    xf = x.astype(jnp.float32)
    y = xf * lax.rsqrt(jnp.mean(xf * xf, axis=-1, keepdims=True) + EPS)
    return (y * gain.astype(jnp.float32)).astype(x.dtype)


def to_heads(t, n_heads):
    b, l, _ = t.shape
    return t.reshape(b, l, n_heads, -1).transpose(0, 2, 1, 3)


def from_heads(t):
    b, h, l, d = t.shape
    return t.transpose(0, 2, 1, 3).reshape(b, l, h * d)


def gla_chunk_scan(q, k, v, log_f, s0):
    b, h, l, _ = q.shape
    n = l // CHUNK

    def chunks(t):
        return jnp.moveaxis(t.reshape(b, h, n, CHUNK, t.shape[-1]), 2, 0)

    causal = jnp.tril(jnp.ones((CHUNK, CHUNK), dtype=bool))[:, :, None]

    def step(state, inp):
        qc, kc, vc, lfc = inp
        cum = jnp.cumsum(lfc, axis=-2)
        o_inter = jnp.einsum('bhtk,bhkv->bhtv', qc * jnp.exp(cum), state)
        diff = cum[:, :, :, None, :] - cum[:, :, None, :, :]
        decay = jnp.exp(jnp.where(causal, diff, -jnp.inf))
        scores = jnp.einsum('bhtsk,bhtk,bhsk->bhts', decay, qc, kc)
        o = o_inter + jnp.einsum('bhts,bhsv->bhtv', scores, vc)
        last = cum[:, :, -1:, :]
        new_state = (jnp.exp(last[:, :, 0, :])[..., None] * state
                     + jnp.einsum('bhsk,bhsv->bhkv', kc * jnp.exp(last - cum), vc))
        return new_state, o

    s_fin, o = lax.scan(step, s0, (chunks(q), chunks(k), chunks(v), chunks(log_f)))
    o = jnp.moveaxis(o, 0, 2).reshape(b, h, l, v.shape[-1])
    return o, s_fin


def hgrn2_mix(z, lb, s0):
    z32 = z.astype(jnp.float32)
    zq, zi, zg, z_fwd, z_bwd = jnp.split(z32, 5, axis=-1)
    q = to_heads(jax.nn.silu(zq), HG_HEADS)
    v = to_heads(zi, HG_HEADS)
    lbh = lb.astype(jnp.float32).reshape(HG_HEADS, 1, HG_DIM)

    def forget(t):
        f = lbh + (1.0 - lbh) * jax.nn.sigmoid(to_heads(t, HG_HEADS))
        return 1.0 - f, jnp.log(f)

    k_f, lf_f = forget(z_fwd)
    k_b, lf_b = forget(z_bwd)
    rev = lambda t: jnp.flip(t, axis=2)
    o_f, s_f = gla_chunk_scan(q, k_f, v, lf_f, s0[:, 0])
    o_b, s_b = gla_chunk_scan(rev(q), rev(k_b), rev(v), rev(lf_b), s0[:, 1])
    o = o_f + rev(o_b)
    o = o * lax.rsqrt(jnp.mean(o * o, axis=-1, keepdims=True) + EPS) * jax.nn.silu(to_heads(zg, HG_HEADS))
    return from_heads(o).astype(z.dtype), jnp.stack([s_f, s_b], axis=1)


def s5_discretize(a_re, a_im, log_dt, b_re, b_im, c_re, c_im):
    f = jnp.float32
    lam = lax.complex(a_re.astype(f), a_im.astype(f))
    dt = jnp.exp(log_dt.astype(f))[..., None]
    abar = jnp.exp(lam * dt)
    bbar = ((abar - 1.0) / lam)[..., None] * lax.complex(b_re.astype(f), b_im.astype(f))
    cmat = lax.complex(c_re.astype(f), c_im.astype(f))
    return abar, bbar, cmat


def diag_scan(bu, abar, h0):
    bu = bu.at[:, 0].add(abar * h0)
    a = jnp.broadcast_to(abar, bu.shape)

    def combine(e1, e2):
        return e2[0] * e1[0], e2[0] * e1[1] + e2[1]

    _, h = lax.associative_scan(combine, (a, bu), axis=1)
    return h


def s5_mix(z, h0, p):
    b, l, _ = z.shape
    u = z.astype(jnp.float32).reshape(b, l, S5_GROUPS, S5_GROUP)
    uc = u.astype(jnp.complex64)
    abar, bbar, cmat = p['abar'], p['bbar'], p['cmat']
    h_f = diag_scan(jnp.einsum('blgn,gpn->blgp', uc, bbar[0]), abar[0], h0[:, 0])
    h_b = diag_scan(jnp.einsum('blgn,gpn->blgp', jnp.flip(uc, axis=1), bbar[1]), abar[1], h0[:, 1])
    y = (jnp.real(jnp.einsum('blgp,gnp->blgn', h_f, cmat[0]))
         + jnp.real(jnp.einsum('blgp,gnp->blgn', jnp.flip(h_b, axis=1), cmat[1]))
         + p['s5_d'].astype(jnp.float32).reshape(S5_GROUPS, S5_GROUP) * u)
    y = jax.nn.gelu(y.reshape(b, l, S5_WIDTH))
    out = y * jax.nn.sigmoid(jnp.dot(y, p['w_glu'].astype(jnp.float32)) + p['b_glu'].astype(jnp.float32))
    return out.astype(z.dtype), jnp.stack([h_f[:, -1], h_b[:, -1]], axis=1)


def context_attention(q, k, v):
    s = jnp.einsum('bhqd,bhkd->bhqk', q, k).astype(jnp.float32) * (q.shape[-1] ** -0.5)
    return jnp.einsum('bhqk,bhkd->bhqd', jax.nn.softmax(s, axis=-1).astype(v.dtype), v)


def neighbourhood_attention(q, k, v, ck, cv, rpb):
    b, h, l, d = q.shape
    rows = l // GRID_W
    kh = min(WIN_ROWS, rows)
    r = jnp.arange(rows)
    row_start = jnp.clip(r - kh // 2, 0, rows - kh)
    key_rows = row_start[:, None] + jnp.arange(kh)[None, :]
    col = jnp.arange(GRID_W)
    col_start = jnp.clip(col - WIN_COLS // 2, 0, GRID_W - WIN_COLS)
    in_win = (col[None, :] >= col_start[:, None]) & (col[None, :] < col_start[:, None] + WIN_COLS)
    d_row = key_rows - r[:, None] + (WIN_ROWS - 1)
    d_col = jnp.clip(col[None, :] - col[:, None] + (WIN_COLS - 1), 0, 2 * WIN_COLS - 2)
    bias = rpb[:, d_row[:, None, :, None], d_col[None, :, None, :]].astype(jnp.float32)
    bias = jnp.where(in_win[:, None, :], bias, -jnp.inf)
    qg = q.reshape(b, h, rows, GRID_W, d)
    kg = k.reshape(b, h, rows, GRID_W, d)[:, :, key_rows]
    vg = v.reshape(b, h, rows, GRID_W, d)[:, :, key_rows]
    scale = d ** -0.5
    s_win = jnp.einsum('bhrqd,bhrjkd->bhrqjk', qg, kg).astype(jnp.float32) * scale + bias
    s_ctx = jnp.einsum('bhrqd,bhcd->bhrqc', qg, ck).astype(jnp.float32) * scale
    n_win = kh * GRID_W
    probs = jax.nn.softmax(jnp.concatenate([s_win.reshape(b, h, rows, GRID_W, n_win), s_ctx], axis=-1),
                           axis=-1).astype(v.dtype)
    p_win = probs[..., :n_win].reshape(b, h, rows, GRID_W, kh, GRID_W)
    out = (jnp.einsum('bhrqjk,bhrjkd->bhrqd', p_win, vg)
           + jnp.einsum('bhrqc,bhcd->bhrqd', probs[..., n_win:], cv.astype(v.dtype)))
    return out.reshape(b, h, l, d)


def expert_layer(x, p):
    shape = x.shape
    xf = x.reshape(-1, shape[-1])
    n_tok = xf.shape[0]
    logits = jnp.dot(xf, p['w_router']).astype(jnp.float32) + p['b_router'].astype(jnp.float32)
    top_val, top_idx = lax.top_k(logits, TOP_K)
    gates = jax.nn.softmax(top_val, axis=-1)
    n_assign = n_tok * TOP_K
    expert_of = top_idx.reshape(-1).astype(jnp.int32)
    token_of = jnp.arange(n_assign, dtype=jnp.int32) // TOP_K
    order = jnp.argsort(expert_of)
    sorted_expert = expert_of[order]
    counts = jnp.bincount(expert_of, length=N_EXPERTS).astype(jnp.int32)
    padded = (counts + EXPERT_BLOCK - 1) // EXPERT_BLOCK * EXPERT_BLOCK
    pad_end = jnp.cumsum(padded)
    pad_start = pad_end - padded
    grp_start = jnp.cumsum(counts) - counts
    rank = jnp.arange(n_assign, dtype=jnp.int32) - grp_start[sorted_expert]
    dest = jnp.zeros((n_assign,), jnp.int32).at[order].set((pad_start[sorted_expert] + rank).astype(jnp.int32))
    n_blocks = -(-n_assign // EXPERT_BLOCK) + N_EXPERTS
    rows_total = n_blocks * EXPERT_BLOCK
    src = jnp.full((rows_total,), n_tok, jnp.int32).at[dest].set(token_of)
    x_buf = jnp.concatenate([xf, jnp.zeros((1, xf.shape[-1]), xf.dtype)], axis=0)[src]
    block_expert = jnp.minimum(
        jnp.searchsorted(pad_end, jnp.arange(n_blocks, dtype=jnp.int32) * EXPERT_BLOCK, side='right'),
        N_EXPERTS - 1)
    w_gate_up, b_gate_up, w_down, b_down = p['w_gate_up'], p['b_gate_up'], p['w_down'], p['b_down']

    def run_block(args):
        xb, e = args
        hgu = (jnp.dot(xb, w_gate_up[e]) + b_gate_up[e]).astype(jnp.float32)
        gate = jnp.minimum(hgu[:, :D_FF], SWIGLU_LIMIT)
        up = jnp.clip(hgu[:, D_FF:], -SWIGLU_LIMIT, SWIGLU_LIMIT)
        act = ((up + 1.0) * gate * jax.nn.sigmoid(SWIGLU_ALPHA * gate)).astype(xb.dtype)
        return jnp.dot(act, w_down[e]) + b_down[e]

    y_buf = lax.map(run_block, (x_buf.reshape(n_blocks, EXPERT_BLOCK, -1), block_expert)).reshape(rows_total, -1)
    y = y_buf[dest].reshape(n_tok, TOP_K, -1)
    out = jnp.einsum('nk,nkd->nd', gates.astype(y.dtype), y)
    return out.reshape(shape)


def mixer(h, p, ctx):
    b = h.shape[0]
    z = jnp.dot(h, p['w_in'])
    z_hg = z[..., :5 * HG_WIDTH]
    z_s5 = z[..., 5 * HG_WIDTH:5 * HG_WIDTH + S5_WIDTH]
    q_na, k_na, v_na = [to_heads(t, NA_HEADS) for t in jnp.split(z[..., 5 * HG_WIDTH + S5_WIDTH:], 3, axis=-1)]
    if ctx is None:
        s0 = jnp.zeros((b, 2, HG_HEADS, HG_DIM, HG_DIM), jnp.float32)
        h0 = jnp.zeros((b, 2, S5_GROUPS, S5_STATE), jnp.complex64)
    else:
        s0, h0, ck, cv = ctx
    hg_out, s_fin = hgrn2_mix(z_hg, p['lb'], s0)
    s5_out, h_fin = s5_mix(z_s5, h0, p)
    if ctx is None:
        na = context_attention(q_na, k_na, v_na)
        new = (s_fin, h_fin, k_na, v_na)
    else:
        na = neighbourhood_attention(q_na, k_na, v_na, ck, cv, p['rpb'])
        new = None
    y = jnp.dot(jnp.concatenate([hg_out, s5_out, from_heads(na).astype(h.dtype)], axis=-1), p['w_out'])
    return y, new


def block(x, mod, p, ctx):
    sh1, sc1, g1, sh2, sc2, g2 = jnp.split(mod.astype(x.dtype), 6, axis=-1)
    h = rmsnorm(x, p['norm'][0]) * (1 + sc1) + sh1
    y, new = mixer(h, p, ctx)
    x = x + g1 * y
    h = rmsnorm(x, p['norm'][1]) * (1 + sc2) + sh2
    x = x + g2 * expert_layer(h, p)
    return x, new


def setup_inputs(seed: int = 0) -> dict:
    key = jax.random.key(seed)
    ks = jax.random.split(key, 34)
    f32 = jnp.float32
    D = D_MODEL
    G, P, NG = S5_GROUPS, S5_STATE, S5_GROUP

    def nrm(k, shape, s):
        return jax.random.normal(k, shape, f32) * s

    return {
        'x_prompt': nrm(ks[0], (BATCH, SEQ, D), 1.0),
        'x_sample': nrm(ks[1], (DEC_BATCH, DEC_SEQ, D), 1.0),
        'state_hgrn': nrm(ks[2], (DEC_BATCH, DEPTH, 2, HG_HEADS, HG_DIM, HG_DIM), 0.5),
        'state_s5_re': nrm(ks[3], (DEC_BATCH, DEPTH, 2, G, P), 0.1),
        'state_s5_im': nrm(ks[4], (DEC_BATCH, DEPTH, 2, G, P), 0.1),
        'cache_k': nrm(ks[5], (DEC_BATCH, DEPTH, NA_HEADS, PAST_LEN, NA_DIM), 1.0),
        'cache_v': nrm(ks[6], (DEC_BATCH, DEPTH, NA_HEADS, PAST_LEN, NA_DIM), 1.0),
        'c': nrm(ks[7], (DEC_BATCH, D), 1.0),
        'c_ctx': nrm(ks[8], (D,), 1.0),
        'w_mod': nrm(ks[9], (DEPTH, D, 6 * D), 0.5 * D ** -0.5),
        'b_mod': nrm(ks[10], (DEPTH, 6 * D), 0.02),
        'norm_gain': 1.0 + nrm(ks[11], (DEPTH, 2, D), 0.02),
        'w_in': nrm(ks[12], (DEPTH, D, IN_WIDTH), D ** -0.5),
        'w_out': nrm(ks[13], (DEPTH, MIX_WIDTH, D), MIX_WIDTH ** -0.5),
        'hg_lb_logits': nrm(ks[14], (DEPTH, HG_WIDTH), 1.0),
        's5_a_re': -0.5 + nrm(ks[15], (DEPTH, 2, G, P), 0.01),
        's5_a_im': math.pi * jnp.arange(P, dtype=f32) + nrm(ks[16], (DEPTH, 2, G, P), 0.01),
        's5_log_dt': jax.random.uniform(ks[17], (DEPTH, 2, G), f32, math.log(1e-3), math.log(1e-1)),
        's5_b_re': nrm(ks[18], (DEPTH, 2, G, P, NG), (2 * NG) ** -0.5),
        's5_b_im': nrm(ks[19], (DEPTH, 2, G, P, NG), (2 * NG) ** -0.5),
        's5_c_re': nrm(ks[20], (DEPTH, 2, G, NG, P), (2 * P) ** -0.5),
        's5_c_im': nrm(ks[21], (DEPTH, 2, G, NG, P), (2 * P) ** -0.5),
        's5_d': nrm(ks[22], (DEPTH, S5_WIDTH), 1.0),
        'w_glu': nrm(ks[23], (DEPTH, S5_WIDTH, S5_WIDTH), S5_WIDTH ** -0.5),
        'b_glu': nrm(ks[24], (DEPTH, S5_WIDTH), 0.02),
        'na_rpb': nrm(ks[25], (DEPTH, NA_HEADS, 2 * WIN_ROWS - 1, 2 * WIN_COLS - 1), 0.1),
        'w_router': nrm(ks[26], (DEPTH, D, N_EXPERTS), D ** -0.5),
        'b_router': nrm(ks[27], (DEPTH, N_EXPERTS), 0.01),
        'w_gate_up': nrm(ks[28], (DEPTH, N_EXPERTS, D, 2 * D_FF), D ** -0.5),
        'b_gate_up': nrm(ks[29], (DEPTH, N_EXPERTS, 2 * D_FF), 0.02),
        'w_down': nrm(ks[30], (DEPTH, N_EXPERTS, D_FF, D), D_FF ** -0.5),
        'b_down': nrm(ks[31], (DEPTH, N_EXPERTS, D), 0.02),
        'final_norm': 1.0 + nrm(ks[32], (D,), 0.02),
    }


def reference(x_prompt, x_sample, state_hgrn, state_s5_re, state_s5_im, cache_k, cache_v, c, c_ctx,
              w_mod, b_mod, norm_gain, w_in, w_out, hg_lb_logits, s5_a_re, s5_a_im, s5_log_dt,
              s5_b_re, s5_b_im, s5_c_re, s5_c_im, s5_d, w_glu, b_glu, na_rpb,
              w_router, b_router, w_gate_up, b_gate_up, w_down, b_down, final_norm):
    sm = jax.nn.softmax(hg_lb_logits.astype(jnp.float32), axis=0)
    lower_bounds = jnp.cumsum(sm, axis=0) - sm[0:1]
    xc, xl = x_prompt, x_sample
    hg_list, s5_list, k_list, v_list = [], [], [], []
    for l in range(DEPTH):
        abar, bbar, cmat = s5_discretize(s5_a_re[l], s5_a_im[l], s5_log_dt[l], s5_b_re[l], s5_b_im[l],
                                         s5_c_re[l], s5_c_im[l])
        p = {'norm': norm_gain[l], 'w_in': w_in[l], 'w_out': w_out[l], 'lb': lower_bounds[l],
             'abar': abar, 'bbar': bbar, 'cmat': cmat, 's5_d': s5_d[l], 'w_glu': w_glu[l], 'b_glu': b_glu[l],
             'rpb': na_rpb[l], 'w_router': w_router[l], 'b_router': b_router[l],
             'w_gate_up': w_gate_up[l], 'b_gate_up': b_gate_up[l], 'w_down': w_down[l], 'b_down': b_down[l]}
        mod_ctx = (jnp.dot(jax.nn.silu(c_ctx), w_mod[l]) + b_mod[l])[None, None, :]
        mod_lat = (jnp.dot(jax.nn.silu(c), w_mod[l]) + b_mod[l])[:, None, :]
        xc, (s_fin, h_fin, k_ctx, v_ctx) = block(xc, mod_ctx, p, None)
        hg_list.append(s_fin)
        s5_list.append(h_fin)
        k_list.append(k_ctx)
        v_list.append(v_ctx)
        ctx = (state_hgrn[:, l].astype(jnp.float32),
               lax.complex(state_s5_re[:, l].astype(jnp.float32), state_s5_im[:, l].astype(jnp.float32)),
               cache_k[:, l], cache_v[:, l])
        xl, _ = block(xl, mod_lat, p, ctx)
    y_prompt = rmsnorm(xc, final_norm)
    y_sample = rmsnorm(xl, final_norm)
    new_state_hgrn = jnp.stack(hg_list, axis=1)
    new_state_s5_re = jnp.stack([jnp.real(t) for t in s5_list], axis=1)
    new_state_s5_im = jnp.stack([jnp.imag(t) for t in s5_list], axis=1)
    new_cache_k = jnp.stack(k_list, axis=1)
    new_cache_v = jnp.stack(v_list, axis=1)
    return (y_prompt, y_sample, new_state_hgrn, new_state_s5_re, new_state_s5_im, new_cache_k, new_cache_v)
```

```python
import functools
import math

import jax
import jax.numpy as jnp
from jax import lax
from jax.experimental import pallas as pl
from jax.experimental.pallas import tpu as pltpu

f32 = jnp.float32
bf16 = jnp.bfloat16
i32 = jnp.int32

D_MODEL = 2048
BATCH = 32
SEQ = 256
DEPTH = 2
DEC_BATCH = 8
DEC_SEQ = 2048
PAST_LEN = 256
GRID_W = 64
HG_DIM = 128
HG_WIDTH = D_MODEL // 4
HG_HEADS = HG_WIDTH // HG_DIM
S5_GROUP = 16
S5_STATE = 64
S5_WIDTH = D_MODEL // 4
S5_GROUPS = S5_WIDTH // S5_GROUP
NA_DIM = 64
NA_WIDTH = D_MODEL // 2
NA_HEADS = NA_WIDTH // NA_DIM
WIN_ROWS = 8
WIN_COLS = 16
MIX_WIDTH = HG_WIDTH + S5_WIDTH + NA_WIDTH
IN_WIDTH = 5 * HG_WIDTH + S5_WIDTH + 3 * NA_WIDTH
CHUNK = 64
N_EXPERTS = 32
TOP_K = 4
D_FF = D_MODEL
SWIGLU_LIMIT = 7.0
SWIGLU_ALPHA = 1.702
EPS = 1e-6

N_CTX = BATCH * SEQ
N_LAT = DEC_BATCH * DEC_SEQ
N_TOK = N_CTX + N_LAT
N_GROUPS = 1 + DEC_BATCH
MOD_ROWS = 16

LANES = 128
SUBLANES = 8
ROW_SLABS = D_MODEL // LANES
VMEM_LIMIT = 56 * 1024 * 1024

TM_NORM = 512
TM_MM = 1024
TN_MM = 1024
TN_MOD = 1024
EXPERT_BM = 512
EXPERT_TN = 256
N_ASSIGN = N_TOK * TOP_K
N_BLOCKS = N_ASSIGN // EXPERT_BM + N_EXPERTS
ROWS_TOTAL = N_BLOCKS * EXPERT_BM
TM_DISPATCH = 256
TM_COMBINE = 128
S5_CHUNK = 64
S5_ROW = S5_CHUNK * S5_GROUP
HG_SUB = 16


def _params(*sem):
    return pltpu.CompilerParams(dimension_semantics=sem, vmem_limit_bytes=VMEM_LIMIT)


def _dot(a, b):
    return jnp.dot(a, b, preferred_element_type=f32)


def _dot_nt(a, b):
    return lax.dot_general(a, b, (((1,), (1,)), ((), ())), preferred_element_type=f32)


def _dot_tn(a, b):
    return lax.dot_general(a, b, (((0,), (0,)), ((), ())), preferred_element_type=f32)


def _split_bf16(x):
    hi = x.astype(bf16)
    lo = (x - hi.astype(f32)).astype(bf16)
    return hi, lo


def _group_of_tile(i, tm):
    n_ctx_tiles = N_CTX // tm
    per_seq = DEC_SEQ // tm
    return jnp.where(i < n_ctx_tiles, 0, 1 + (i - n_ctx_tiles) // per_seq)


def _mod_body(c_ref, w_ref, b_ref, o_ref):
    c = c_ref[...]
    a_hi, a_lo = _split_bf16(c * jax.nn.sigmoid(c))
    w_hi, w_lo = _split_bf16(w_ref[...])
    o_ref[...] = _dot(a_hi, w_hi) + _dot(a_lo, w_hi) + _dot(a_hi, w_lo) + b_ref[...]


def _modulation(c_all, w_mod, b_mod):
    n = 6 * D_MODEL
    return pl.pallas_call(
        _mod_body,
        grid=(DEPTH, n // TN_MOD),
        in_specs=[
            pl.BlockSpec((MOD_ROWS, D_MODEL), lambda l, j: (0, 0)),
            pl.BlockSpec((None, D_MODEL, TN_MOD), lambda l, j: (l, 0, j)),
            pl.BlockSpec((None, 1, TN_MOD), lambda l, j: (l, 0, j)),
        ],
        out_specs=pl.BlockSpec((None, MOD_ROWS, TN_MOD), lambda l, j: (l, 0, j)),
        out_shape=jax.ShapeDtypeStruct((DEPTH, MOD_ROWS, n), f32),
        compiler_params=_params("arbitrary", "arbitrary"),
        name="modulation",
    )(c_all, w_mod, b_mod.reshape(DEPTH, 1, n))


def _mod_spec(layer, which, tm):
    return pl.BlockSpec(
        (None, 1, D_MODEL),
        lambda i, *_: ((layer * MOD_ROWS + _group_of_tile(i, tm)) * 6 + which, 0, 0))


def _rms_mod(x, gain, scale, shift):
    y = x * lax.rsqrt(jnp.mean(x * x, axis=-1, keepdims=True) + EPS) * gain
    return y * (1.0 + scale) + shift


def _norm_body(x_ref, g_ref, sc_ref, sh_ref, o_ref):
    o_ref[...] = _rms_mod(x_ref[...], g_ref[...], sc_ref[...], sh_ref[...]).astype(o_ref.dtype)


def _norm_mod(x, gain, mod_tab, layer):
    tm = TM_NORM
    return pl.pallas_call(
        _norm_body,
        grid=(N_TOK // tm,),
        in_specs=[
            pl.BlockSpec((tm, D_MODEL), lambda i: (i, 0)),
            pl.BlockSpec((1, D_MODEL), lambda i: (0, 0)),
            _mod_spec(layer, 1, tm),
            _mod_spec(layer, 0, tm),
        ],
        out_specs=pl.BlockSpec((tm, D_MODEL), lambda i: (i, 0)),
        out_shape=jax.ShapeDtypeStruct((N_TOK, D_MODEL), bf16),
        compiler_params=_params("arbitrary"),
        name="norm_mod",
    )(x, gain.reshape(1, D_MODEL), mod_tab, mod_tab)


def _mm_body(a_ref, w_ref, o_ref):
    o_ref[...] = _dot(a_ref[...], w_ref[...])


def _in_proj(h, w_in_bf, layer):
    return pl.pallas_call(
        _mm_body,
        grid=(N_TOK // TM_MM, IN_WIDTH // TN_MM),
        in_specs=[
            pl.BlockSpec((TM_MM, D_MODEL), lambda i, j: (i, 0)),
            pl.BlockSpec((None, D_MODEL, TN_MM), lambda i, j: (layer, 0, j)),
        ],
        out_specs=pl.BlockSpec((TM_MM, TN_MM), lambda i, j: (i, j)),
        out_shape=jax.ShapeDtypeStruct((N_TOK, IN_WIDTH), f32),
        compiler_params=_params("arbitrary", "arbitrary"),
        name="in_proj",
    )(h, w_in_bf)


def _hgrn_chunk(q, k, v, lf, st, rev, tri):
    c = CHUNK
    g = HG_SUB
    lf_hi = lf.astype(bf16)
    r1 = lf - lf_hi.astype(f32)
    lf_mid = r1.astype(bf16)
    lf_lo = (r1 - lf_mid.astype(f32)).astype(bf16)
    cum = _dot(tri, lf_hi) + _dot(tri, lf_mid) + _dot(tri, lf_lo)
    last = cum[0:1] if rev else cum[c - 1:c]

    o = _dot_nt((q * jnp.exp(cum)).astype(bf16), st.astype(bf16))
    kdec = (k * jnp.exp(last - cum)).astype(bf16)
    v_bf = v.astype(bf16)
    st_new = st * jnp.exp(last) + _dot_tn(v_bf, kdec)

    row = lax.broadcasted_iota(i32, (c, LANES), 0)
    sub_row = lax.broadcasted_iota(i32, (SUBLANES, LANES), 0)
    n_sub = c // g
    score_rows = []
    diag_rows = []
    for i in range(n_sub):
        lo, hi = i * g, (i + 1) * g
        if rev:
            bnd = cum[hi:hi + 1] if i < n_sub - 1 else None
            others = row >= hi
            has_others = i < n_sub - 1
        else:
            bnd = cum[lo - 1:lo] if i > 0 else None
            others = row < lo
            has_others = i > 0
        cum_i = cum[lo:hi]
        q_i = q[lo:hi]
        if has_others:
            qt = (q_i * jnp.exp(cum_i - bnd)).astype(bf16)
            kh = (k * jnp.exp(jnp.where(others, bnd - cum, -jnp.inf))).astype(bf16)
            score_rows.append(_dot_nt(qt, kh))
        else:
            score_rows.append(jnp.zeros((g, c), f32))
        halves = [jnp.zeros((SUBLANES, LANES), f32) for _ in range(g // SUBLANES)]
        for s in range(g):
            cs = cum[lo + s:lo + s + 1]
            ks = k[lo + s:lo + s + 1]
            vs = v[lo + s:lo + s + 1]
            s_half, s_in = divmod(s, SUBLANES)
            for hh in range(g // SUBLANES):
                if (hh > s_half) if rev else (hh < s_half):
                    continue
                rows = slice(lo + hh * SUBLANES, lo + (hh + 1) * SUBLANES)
                diff = cum[rows] - cs
                if hh == s_half:
                    valid = (sub_row <= s_in) if rev else (sub_row >= s_in)
                    diff = jnp.where(valid, diff, -jnp.inf)
                w = jnp.sum(q[rows] * ks * jnp.exp(diff), axis=-1, keepdims=True)
                halves[hh] = halves[hh] + w * vs
        diag_rows.extend(halves)
    scores = jnp.concatenate(score_rows, axis=0).astype(bf16)
    o = o + _dot(scores, v_bf) + jnp.concatenate(diag_rows, axis=0)
    return o, st_new


def _hgrn_body(*refs, seq_len, has_state, emit_state):
    zq_ref, zi_ref, zg_ref, zf_ref, zb_ref, lb_ref = refs[:6]
    pos = 6
    s0_ref = None
    if has_state:
        s0_ref = refs[pos]
        pos += 1
    o_ref = refs[pos]
    pos += 1
    sfin_ref = None
    if emit_state:
        sfin_ref = refs[pos]
        pos += 1
    of_scr = refs[pos]

    n_chunks = seq_len // CHUNK
    lb = lb_ref[...]
    rr = lax.broadcasted_iota(i32, (CHUNK, CHUNK), 0)
    cc = lax.broadcasted_iota(i32, (CHUNK, CHUNK), 1)
    tri_f = (cc <= rr).astype(bf16)
    tri_b = (cc >= rr).astype(bf16)

    def load(c, rev):
        r0 = pl.multiple_of(c * CHUNK, CHUNK)
        zq = zq_ref[pl.ds(r0, CHUNK), :]
        q = zq * jax.nn.sigmoid(zq)
        v = zi_ref[pl.ds(r0, CHUNK), :]
        zf = (zb_ref if rev else zf_ref)[pl.ds(r0, CHUNK), :]
        f = lb + (1.0 - lb) * jax.nn.sigmoid(zf)
        return r0, q, 1.0 - f, v, jnp.log(f)

    def fwd_step(c, st):
        r0, q, k, v, lf = load(c, False)
        o, st = _hgrn_chunk(q, k, v, lf, st, False, tri_f)
        of_scr[pl.ds(r0, CHUNK), :] = o
        return st

    def bwd_step(j, st):
        c = n_chunks - 1 - j
        r0, q, k, v, lf = load(c, True)
        o, st = _hgrn_chunk(q, k, v, lf, st, True, tri_b)
        o = o + of_scr[pl.ds(r0, CHUNK), :]
        zg = zg_ref[pl.ds(r0, CHUNK), :]
        o = o * lax.rsqrt(jnp.mean(o * o, axis=-1, keepdims=True) + EPS) * (zg * jax.nn.sigmoid(zg))
        o_ref[pl.ds(r0, CHUNK), :] = o.astype(o_ref.dtype)
        return st

    if has_state:
        st_f0 = s0_ref[0].T
        st_b0 = s0_ref[1].T
    else:
        st_f0 = jnp.zeros((HG_DIM, HG_DIM), f32)
        st_b0 = st_f0
    st_f = lax.fori_loop(0, n_chunks, fwd_step, st_f0)
    st_b = lax.fori_loop(0, n_chunks, bwd_step, st_b0)
    if emit_state:
        sfin_ref[0] = st_f.T
        sfin_ref[1] = st_b.T


def _hgrn(z, lb, state, layer, *, latent):
    if latent:
        nb, seq_len, blk0 = DEC_BATCH, DEC_SEQ, N_CTX // DEC_SEQ
    else:
        nb, seq_len, blk0 = BATCH, SEQ, 0
    heads = HG_HEADS

    def zspec(part):
        return pl.BlockSpec((seq_len, HG_DIM), lambda b, h: (blk0 + b, part * heads + h))

    in_specs = [zspec(0), zspec(1), zspec(2), zspec(3), zspec(4),
                pl.BlockSpec((1, HG_DIM), lambda b, h: (0, h))]
    args = [z, z, z, z, z, lb.reshape(1, HG_WIDTH)]
    if latent:
        in_specs.append(pl.BlockSpec((None, None, 2, None, HG_DIM, HG_DIM),
                                     lambda b, h: (b, layer, 0, h, 0, 0)))
        args.append(state)
    out_specs = [pl.BlockSpec((seq_len, HG_DIM), lambda b, h: (b, h))]
    out_shape = [jax.ShapeDtypeStruct((nb * seq_len, HG_WIDTH), bf16)]
    if not latent:
        out_specs.append(pl.BlockSpec((None, 2, None, HG_DIM, HG_DIM), lambda b, h: (b, 0, h, 0, 0)))
        out_shape.append(jax.ShapeDtypeStruct((nb, 2, heads, HG_DIM, HG_DIM), f32))
    res = pl.pallas_call(
        functools.partial(_hgrn_body, seq_len=seq_len, has_state=latent, emit_state=not latent),
        grid=(nb, heads),
        in_specs=in_specs,
        out_specs=out_specs,
        out_shape=out_shape,
        scratch_shapes=[pltpu.VMEM((seq_len, HG_DIM), f32)],
        compiler_params=_params("arbitrary", "arbitrary"),
        name="hgrn_lat" if latent else "hgrn_ctx",
    )(*args)
    return (res[0], None) if latent else (res[0], res[1])


def _s5_tables(a_re, a_im, log_dt, b_re, b_im, c_re, c_im, d_skip):
    C = S5_CHUNK
    dt = jnp.exp(log_dt)[..., None]
    lam_re, lam_im = a_re * dt, a_im * dt
    j = jnp.arange(C + 1, dtype=f32)[:, None, None, None]
    mag = jnp.exp(j * lam_re[None])
    pw_re = mag * jnp.cos(j * lam_im[None])
    pw_im = mag * jnp.sin(j * lam_im[None])
    ab_re, ab_im = pw_re[1], pw_im[1]
    den = a_re * a_re + a_im * a_im
    n_re, n_im = ab_re - 1.0, ab_im
    g_re = (n_re * a_re + n_im * a_im) / den
    g_im = (n_im * a_re - n_re * a_im) / den
    bb_re = g_re[..., None] * b_re - g_im[..., None] * b_im
    bb_im = g_re[..., None] * b_im + g_im[..., None] * b_re
    cp_re = (c_re[None] * pw_re[:C, :, :, None, :] - c_im[None] * pw_im[:C, :, :, None, :])
    cp_im = (c_re[None] * pw_im[:C, :, :, None, :] + c_im[None] * pw_re[:C, :, :, None, :])
    kern = (jnp.einsum('jdgnp,dgpm->djgnm', cp_re, bb_re, precision='highest')
            - jnp.einsum('jdgnp,dgpm->djgnm', cp_im, bb_im, precision='highest'))
    s_idx = jnp.arange(C)[:, None]
    t_idx = jnp.arange(C)[None, :]
    lag_f = jnp.clip(t_idx - s_idx, 0, C - 1)
    lag_b = jnp.clip(s_idx - t_idx, 0, C - 1)
    kf = jnp.where((t_idx >= s_idx)[:, :, None, None, None], kern[0][lag_f], 0.0)
    kb = jnp.where((s_idx >= t_idx)[:, :, None, None, None], kern[1][lag_b], 0.0)
    eye_t = (s_idx == t_idx).astype(f32)[:, :, None, None, None]
    eye_n = jnp.eye(S5_GROUP, dtype=f32)[None, None, None]
    dmat = eye_t * eye_n * d_skip.reshape(S5_GROUPS, S5_GROUP)[None, None, :, :, None]
    m_all = (kf + kb + dmat).transpose(2, 0, 4, 1, 3).reshape(S5_GROUPS, S5_ROW, S5_ROW)
    s_ar = jnp.arange(C)
    pf_re, pf_im = pw_re[C - 1 - s_ar, 0], pw_im[C - 1 - s_ar, 0]
    pb_re, pb_im = pw_re[s_ar, 1], pw_im[s_ar, 1]
    def w_part(p_re, p_im, d):
        re = p_re[..., None] * bb_re[d][None] - p_im[..., None] * bb_im[d][None]
        im = p_re[..., None] * bb_im[d][None] + p_im[..., None] * bb_re[d][None]
        return re.transpose(1, 0, 3, 2).reshape(S5_GROUPS, S5_ROW, S5_STATE), \
            im.transpose(1, 0, 3, 2).reshape(S5_GROUPS, S5_ROW, S5_STATE)
    wf_re, wf_im = w_part(pf_re, pf_im, 0)
    wb_re, wb_im = w_part(pb_re, pb_im, 1)
    w_all = jnp.concatenate([wf_re, wf_im, wb_re, wb_im], axis=-1)
    t_ar = jnp.arange(C)
    def v_part(p_re, p_im, d):
        re = c_re[d][None] * p_re[:, :, None, :] - c_im[d][None] * p_im[:, :, None, :]
        im = c_re[d][None] * p_im[:, :, None, :] + c_im[d][None] * p_re[:, :, None, :]
        top = re.transpose(1, 3, 0, 2).reshape(S5_GROUPS, S5_STATE, S5_ROW)
        bot = (-im).transpose(1, 3, 0, 2).reshape(S5_GROUPS, S5_STATE, S5_ROW)
        return jnp.concatenate([top, bot], axis=1)
    vf = v_part(pw_re[t_ar + 1, 0], pw_im[t_ar + 1, 0], 0)
    vb = v_part(pw_re[C - t_ar, 1], pw_im[C - t_ar, 1], 1)
    def step_vecs(d):
        ar, ai = pw_re[C, d], pw_im[C, d]
        return jnp.concatenate([ar, ar], -1), jnp.concatenate([-ai, ai], -1)
    a1f, a2f = step_vecs(0)
    a1b, a2b = step_vecs(1)
    zeros = jnp.zeros_like(a1f)
    step = jnp.stack([a1f, a2f, a1b, a2b, zeros, zeros, zeros, zeros], axis=1)
    return m_all.astype(bf16), w_all.astype(bf16), vf.astype(bf16), vb.astype(bf16), step


def _s5_body(u_ref, m_ref, w_ref, vf_ref, vb_ref, step_ref, h0_ref, y_ref, hfin_ref,
             e_scr, hp_scr, hn_scr, *, n_chunks, n_batch):
    u = u_ref[...].astype(bf16)
    e_scr[...] = _dot(u, w_ref[...])
    a1f, a2f = step_ref[0:1, :], step_ref[1:2, :]
    a1b, a2b = step_ref[2:3, :], step_ref[3:4, :]
    two_p = 2 * S5_STATE

    def cmul(h, a1, a2):
        return a1 * h + a2 * pltpu.roll(h, S5_STATE, 1)

    def fstep(c, h):
        r0 = pl.multiple_of(c * n_batch, n_batch)
        hp_scr[pl.ds(r0, n_batch), :] = h
        return cmul(h, a1f, a2f) + e_scr[pl.ds(r0, n_batch), 0:two_p]

    def bstep(j, h):
        r0 = pl.multiple_of((n_chunks - 1 - j) * n_batch, n_batch)
        hn_scr[pl.ds(r0, n_batch), :] = h
        return cmul(h, a1b, a2b) + e_scr[pl.ds(r0, n_batch), two_p:2 * two_p]

    hfin_ref[0] = lax.fori_loop(0, n_chunks, fstep, h0_ref[0])
    hfin_ref[1] = lax.fori_loop(0, n_chunks, bstep, h0_ref[1])
    y_ref[...] = (_dot(u, m_ref[...]) + _dot(hp_scr[...].astype(bf16), vf_ref[...])
                  + _dot(hn_scr[...].astype(bf16), vb_ref[...]))


def _s5_scan(u, h0, tables, n_chunks, n_batch, name):
    m_all, w_all, vf, vb, step = tables
    rows = n_chunks * n_batch
    two_p = 2 * S5_STATE
    return pl.pallas_call(
        functools.partial(_s5_body, n_chunks=n_chunks, n_batch=n_batch),
        grid=(S5_GROUPS,),
        in_specs=[
            pl.BlockSpec((None, rows, S5_ROW), lambda g: (g, 0, 0)),
            pl.BlockSpec((None, S5_ROW, S5_ROW), lambda g: (g, 0, 0)),
            pl.BlockSpec((None, S5_ROW, 2 * two_p), lambda g: (g, 0, 0)),
            pl.BlockSpec((None, two_p, S5_ROW), lambda g: (g, 0, 0)),
            pl.BlockSpec((None, two_p, S5_ROW), lambda g: (g, 0, 0)),
            pl.BlockSpec((None, SUBLANES, two_p), lambda g: (g, 0, 0)),
            pl.BlockSpec((None, 2, n_batch, two_p), lambda g: (g, 0, 0, 0)),
        ],
        out_specs=[
            pl.BlockSpec((None, rows, S5_ROW), lambda g: (g, 0, 0)),
            pl.BlockSpec((None, 2, n_batch, two_p), lambda g: (g, 0, 0, 0)),
        ],
        out_shape=[
            jax.ShapeDtypeStruct((S5_GROUPS, rows, S5_ROW), f32),
            jax.ShapeDtypeStruct((S5_GROUPS, 2, n_batch, two_p), f32),
        ],
        scratch_shapes=[
            pltpu.VMEM((rows, 2 * two_p), f32),
            pltpu.VMEM((rows, two_p), f32),
            pltpu.VMEM((rows, two_p), f32),
        ],
        compiler_params=_params("arbitrary"),
        name=name,
    )(u, m_all, w_all, vf, vb, step, h0)


def _s5_rows(z_part, n_batch, seq_len):
    nc = seq_len // S5_CHUNK
    u = z_part.reshape(n_batch, nc, S5_CHUNK, S5_GROUPS, S5_GROUP)
    return u.transpose(3, 1, 0, 2, 4).reshape(S5_GROUPS, nc * n_batch, S5_ROW)


def _s5_unrows(y, n_batch, seq_len):
    nc = seq_len // S5_CHUNK
    y = y.reshape(S5_GROUPS, nc, n_batch, S5_CHUNK, S5_GROUP)
    return y.transpose(2, 1, 3, 0, 4).reshape(n_batch * seq_len, S5_WIDTH)


def _glu_body(y_ref, w_ref, b_ref, o_ref):
    y = jax.nn.gelu(y_ref[...])
    gate = jax.nn.sigmoid(_dot(y.astype(bf16), w_ref[...]) + b_ref[...])
    o_ref[...] = (y * gate).astype(o_ref.dtype)


def _s5_glu(y, w_glu_bf, b_glu, layer):
    tm = TM_MM
    return pl.pallas_call(
        _glu_body,
        grid=(N_TOK // tm,),
        in_specs=[
            pl.BlockSpec((tm, S5_WIDTH), lambda i: (i, 0)),
            pl.BlockSpec((None, S5_WIDTH, S5_WIDTH), lambda i: (layer, 0, 0)),
            pl.BlockSpec((None, 1, S5_WIDTH), lambda i: (layer, 0, 0)),
        ],
        out_specs=pl.BlockSpec((tm, S5_WIDTH), lambda i: (i, 0)),
        out_shape=jax.ShapeDtypeStruct((N_TOK, S5_WIDTH), bf16),
        compiler_params=_params("arbitrary"),
        name="s5_glu",
    )(y, w_glu_bf, b_glu.reshape(DEPTH, 1, S5_WIDTH))


def _ctx_attn_body(q_ref, k_ref, v_ref, o_ref, ko_ref, vo_ref):
    scale = NA_DIM ** -0.5
    for h in range(NA_HEADS):
        cols = slice(h * NA_DIM, (h + 1) * NA_DIM)
        q = q_ref[:, cols]
        k = k_ref[:, cols]
        v = v_ref[:, cols]
        ko_ref[h] = k
        vo_ref[h] = v
        s = _dot_nt(q.astype(bf16), k.astype(bf16)) * scale
        p = jnp.exp(s - jnp.max(s, axis=-1, keepdims=True))
        p = p / jnp.sum(p, axis=-1, keepdims=True)
        o_ref[:, cols] = _dot(p.astype(bf16), v.astype(bf16)).astype(o_ref.dtype)


def _ctx_attention(z):
    c0 = (5 * HG_WIDTH + S5_WIDTH) // NA_WIDTH

    def zspec(part):
        return pl.BlockSpec((SEQ, NA_WIDTH), lambda b: (b, c0 + part))

    cache_spec = pl.BlockSpec((None, NA_HEADS, SEQ, NA_DIM), lambda b: (b, 0, 0, 0))
    cache_shape = jax.ShapeDtypeStruct((BATCH, NA_HEADS, SEQ, NA_DIM), f32)
    return pl.pallas_call(
        _ctx_attn_body,
        grid=(BATCH,),
        in_specs=[zspec(0), zspec(1), zspec(2)],
        out_specs=[pl.BlockSpec((SEQ, NA_WIDTH), lambda b: (b, 0)), cache_spec, cache_spec],
        out_shape=[jax.ShapeDtypeStruct((N_CTX, NA_WIDTH), bf16), cache_shape, cache_shape],
        compiler_params=_params("arbitrary"),
        name="ctx_attention",
    )(z, z, z)


def _na_bias_table(rpb):
    col = jnp.arange(GRID_W)
    col_start = jnp.clip(col - WIN_COLS // 2, 0, GRID_W - WIN_COLS)
    in_win = (col[None, :] >= col_start[:, None]) & (col[None, :] < col_start[:, None] + WIN_COLS)
    d_col = jnp.clip(col[None, :] - col[:, None] + (WIN_COLS - 1), 0, 2 * WIN_COLS - 2)
    tab = jnp.where(in_win[None, None], rpb.astype(f32)[:, :, d_col], -jnp.inf)
    return jnp.concatenate([tab[:, :-1], tab[:, 1:]], axis=-1)


def _na_body(q_ref, k_ref, v_ref, ck_ref, cv_ref, bias_ref, o_ref):
    rows = DEC_SEQ // GRID_W
    kh = min(WIN_ROWS, rows)
    scale = NA_DIM ** -0.5
    n_win = kh * GRID_W
    for hh in range(LANES // NA_DIM):
        cols = slice(hh * NA_DIM, (hh + 1) * NA_DIM)
        ck = ck_ref[hh].astype(bf16)
        cv = cv_ref[hh].astype(bf16)

        def row_step(r, carry):
            row_start = jnp.clip(r - kh // 2, 0, rows - kh)
            q0 = pl.multiple_of(r * GRID_W, GRID_W)
            k0 = pl.multiple_of(row_start * GRID_W, GRID_W)
            q = q_ref[pl.ds(q0, GRID_W), cols].astype(bf16)
            kw = k_ref[pl.ds(k0, n_win), cols].astype(bf16)
            vw = v_ref[pl.ds(k0, n_win), cols].astype(bf16)
            d_row0 = row_start - r + (WIN_ROWS - 1)
            bias = jnp.concatenate([bias_ref[hh, d_row0 + j] for j in range(0, kh, 2)], axis=-1)
            s_win = _dot_nt(q, kw) * scale + bias
            s_ctx = _dot_nt(q, ck) * scale
            m = jnp.maximum(jnp.max(s_win, axis=-1, keepdims=True), jnp.max(s_ctx, axis=-1, keepdims=True))
            p_win = jnp.exp(s_win - m)
            p_ctx = jnp.exp(s_ctx - m)
            den = jnp.sum(p_win, axis=-1, keepdims=True) + jnp.sum(p_ctx, axis=-1, keepdims=True)
            out = _dot((p_win / den).astype(bf16), vw) + _dot((p_ctx / den).astype(bf16), cv)
            o_ref[pl.ds(q0, GRID_W), cols] = out.astype(o_ref.dtype)
            return carry

        lax.fori_loop(0, rows, row_step, 0)


def _neighbourhood_attention(z, cache_k, cache_v, bias_tab, layer):
    hp = LANES // NA_DIM
    c0 = (5 * HG_WIDTH + S5_WIDTH) // LANES
    per = NA_WIDTH // LANES
    blk0 = N_CTX // DEC_SEQ

    def zspec(part):
        return pl.BlockSpec((DEC_SEQ, LANES), lambda b, g: (blk0 + b, c0 + part * per + g))

    cache_spec = pl.BlockSpec((None, None, hp, PAST_LEN, NA_DIM), lambda b, g: (b, layer, g, 0, 0))
    return pl.pallas_call(
        _na_body,
        grid=(DEC_BATCH, NA_HEADS // hp),
        in_specs=[zspec(0), zspec(1), zspec(2), cache_spec, cache_spec,
                  pl.BlockSpec((hp, 2 * WIN_ROWS - 2, GRID_W, 2 * GRID_W), lambda b, g: (g, 0, 0, 0))],
        out_specs=pl.BlockSpec((DEC_SEQ, LANES), lambda b, g: (b, g)),
        out_shape=jax.ShapeDtypeStruct((N_LAT, NA_WIDTH), bf16),
        compiler_params=_params("arbitrary", "arbitrary"),
        name="neighbourhood_attention",
    )(z, z, z, cache_k, cache_v, bias_tab)


def _out_proj_body(hg_ref, s5_ref, na_ref, w_ref, x_ref, gate_ref, o_ref):
    y = (_dot(hg_ref[...], w_ref[0:HG_WIDTH, :])
         + _dot(s5_ref[...], w_ref[HG_WIDTH:HG_WIDTH + S5_WIDTH, :])
         + _dot(na_ref[...], w_ref[HG_WIDTH + S5_WIDTH:MIX_WIDTH, :]))
    o_ref[...] = x_ref[...] + gate_ref[...] * y


def _out_proj(hg, s5, na, w_out_bf, x, mod_tab, layer):
    tm = TM_NORM
    return pl.pallas_call(
        _out_proj_body,
        grid=(N_TOK // tm,),
        in_specs=[
            pl.BlockSpec((tm, HG_WIDTH), lambda i: (i, 0)),
            pl.BlockSpec((tm, S5_WIDTH), lambda i: (i, 0)),
            pl.BlockSpec((tm, NA_WIDTH), lambda i: (i, 0)),
            pl.BlockSpec((None, MIX_WIDTH, D_MODEL), lambda i: (layer, 0, 0)),
            pl.BlockSpec((tm, D_MODEL), lambda i: (i, 0)),
            _mod_spec(layer, 2, tm),
        ],
        out_specs=pl.BlockSpec((tm, D_MODEL), lambda i: (i, 0)),
        out_shape=jax.ShapeDtypeStruct((N_TOK, D_MODEL), f32),
        compiler_params=_params("arbitrary"),
        name="out_proj",
    )(hg, s5, na, w_out_bf, x, mod_tab)


def _norm_router_body(x_ref, g_ref, sc_ref, sh_ref, wr_ref, br_ref, upper_ref,
                      h_ref, idx_ref, gate_ref, rank_ref, cnt_ref, base_scr):
    tm = TM_NORM
    i = pl.program_id(0)

    @pl.when(i == 0)
    def _():
        base_scr[...] = jnp.zeros_like(base_scr)

    h = _rms_mod(x_ref[...], g_ref[...], sc_ref[...], sh_ref[...])
    for c in range(ROW_SLABS):
        h_ref[pl.ds(c, tm, stride=ROW_SLABS), :] = h[:, c * LANES:(c + 1) * LANES]

    h_hi, h_lo = _split_bf16(h)
    w = wr_ref[...]
    t1 = _dot_nt(w, h_hi)
    t2 = _dot_nt(w[0:N_EXPERTS], h_lo)
    logits = t1[0:N_EXPERTS] + t1[N_EXPERTS:2 * N_EXPERTS] + t2 + br_ref[...]

    eio = lax.broadcasted_iota(i32, (N_EXPERTS, tm), 0).astype(f32)
    work = logits
    vals, onehots = [], []
    for k in range(TOP_K):
        m = jnp.max(work, axis=0, keepdims=True)
        ik = jnp.min(jnp.where(work == m, eio, float(N_EXPERTS)), axis=0, keepdims=True)
        sel = eio == ik
        work = jnp.where(sel, -jnp.inf, work)
        vals.append(m)
        onehots.append(sel.astype(f32))
        idx_ref[k:k + 1, :] = ik.astype(i32)
    exps = [jnp.exp(v - vals[0]) for v in vals]
    den = exps[0] + exps[1] + exps[2] + exps[3]
    for k in range(TOP_K):
        gate_ref[k:k + 1, :] = exps[k] / den

    oh_all = jnp.concatenate(onehots, axis=0)
    prefix = _dot(oh_all.astype(bf16), upper_ref[...])
    base = base_scr[:, 0:1]
    for k in range(TOP_K):
        oh = onehots[k]
        pos = base + prefix[k * N_EXPERTS:(k + 1) * N_EXPERTS]
        rank_ref[k:k + 1, :] = jnp.sum(oh * pos, axis=0, keepdims=True).astype(i32)
        base = base + jnp.sum(oh, axis=1, keepdims=True)
    base_scr[...] = jnp.broadcast_to(base, base_scr.shape)
    cnt_ref[...] = jnp.broadcast_to(base, cnt_ref.shape)


def _norm_router(x, gain, mod_tab, wr_t, b_router, upper, layer):
    tm = TM_NORM
    return pl.pallas_call(
        _norm_router_body,
        grid=(N_TOK // tm,),
        in_specs=[
            pl.BlockSpec((tm, D_MODEL), lambda i: (i, 0)),
            pl.BlockSpec((1, D_MODEL), lambda i: (0, 0)),
            _mod_spec(layer, 4, tm),
            _mod_spec(layer, 3, tm),
            pl.BlockSpec((None, 2 * N_EXPERTS, D_MODEL), lambda i: (layer, 0, 0)),
            pl.BlockSpec((None, N_EXPERTS, 1), lambda i: (layer, 0, 0)),
            pl.BlockSpec((tm, tm), lambda i: (0, 0)),
        ],
        out_specs=[
            pl.BlockSpec((tm * ROW_SLABS, LANES), lambda i: (i, 0)),
            pl.BlockSpec((TOP_K, tm), lambda i: (0, i)),
            pl.BlockSpec((TOP_K, tm), lambda i: (0, i)),
            pl.BlockSpec((TOP_K, tm), lambda i: (0, i)),
            pl.BlockSpec((N_EXPERTS, LANES), lambda i: (0, 0)),
        ],
        out_shape=[
            jax.ShapeDtypeStruct((N_TOK * ROW_SLABS, LANES), f32),
            jax.ShapeDtypeStruct((TOP_K, N_TOK), i32),
            jax.ShapeDtypeStruct((TOP_K, N_TOK), f32),
            jax.ShapeDtypeStruct((TOP_K, N_TOK), i32),
            jax.ShapeDtypeStruct((N_EXPERTS, LANES), f32),
        ],
        scratch_shapes=[pltpu.VMEM((N_EXPERTS, LANES), f32)],
        compiler_params=_params("arbitrary"),
        name="norm_router",
    )(x, gain.reshape(1, D_MODEL), mod_tab, mod_tab, wr_t, b_router.reshape(DEPTH, N_EXPERTS, 1), upper)


def _dispatch_body(dest_ref, cnt_ref, start_ref, h_hbm, zero_hbm, xb_hbm, sem):
    tm = TM_DISPATCH
    i = pl.program_id(0)
    n = TOP_K * tm

    def issue(a, c):
        t = a & (tm - 1)
        pltpu.make_async_copy(h_hbm.at[i * tm + t], xb_hbm.at[dest_ref[0, 0, a]], sem).start()
        return c

    lax.fori_loop(0, n, issue, 0)

    def wait_one(a, c):
        pltpu.make_async_copy(h_hbm.at[0], xb_hbm.at[0], sem).wait()
        return c

    lax.fori_loop(0, n, wait_one, 0)

    @pl.when(i == 0)
    def _():
        def per_expert(e, c):
            cnt = cnt_ref[e]
            padded = (cnt + (EXPERT_BM - 1)) // EXPERT_BM * EXPERT_BM
            base = start_ref[e]

            def fill(r, c2):
                pltpu.make_async_copy(zero_hbm.at[0], xb_hbm.at[base + r], sem).start()
                return c2

            lax.fori_loop(cnt, padded, fill, 0)

            def drain(r, c2):
                pltpu.make_async_copy(zero_hbm.at[0], xb_hbm.at[0], sem).wait()
                return c2

            lax.fori_loop(cnt, padded, drain, 0)
            return c

        lax.fori_loop(0, N_EXPERTS, per_expert, 0)


def _dispatch(h_rows, dest_tiles, counts, pad_start):
    tm = TM_DISPATCH
    zero_row = jnp.zeros((1, ROW_SLABS, LANES), f32)
    return pl.pallas_call(
        _dispatch_body,
        grid=(N_TOK // tm,),
        in_specs=[
            pl.BlockSpec((1, 1, TOP_K * tm), lambda i: (i, 0, 0), memory_space=pltpu.SMEM),
            pl.BlockSpec(memory_space=pltpu.SMEM),
            pl.BlockSpec(memory_space=pltpu.SMEM),
            pl.BlockSpec(memory_space=pl.ANY),
            pl.BlockSpec(memory_space=pl.ANY),
        ],
        out_specs=pl.BlockSpec(memory_space=pl.ANY),
        out_shape=jax.ShapeDtypeStruct((ROWS_TOTAL, ROW_SLABS, LANES), f32),
        scratch_shapes=[pltpu.SemaphoreType.DMA(())],
        compiler_params=_params("arbitrary"),
        name="dispatch",
    )(dest_tiles, counts, pad_start, h_rows, zero_row)


def _gmm_body(be_ref, nu_ref, x_ref, wg_ref, wu_ref, wd_ref, bg_ref, bu_ref, bd_ref, o_ref, x_scr, acc_scr):
    bm = EXPERT_BM
    i = pl.program_id(0)
    j = pl.program_id(1)
    n_j = pl.num_programs(1)

    @pl.when(i < nu_ref[0])
    def _():
        @pl.when(j == 0)
        def _():
            for c in range(ROW_SLABS):
                x_scr[:, c * LANES:(c + 1) * LANES] = x_ref[pl.ds(c, bm, stride=ROW_SLABS), :].astype(bf16)

        x = x_scr[...]
        gate = _dot(x, wg_ref[...].astype(bf16)) + bg_ref[...]
        up = _dot(x, wu_ref[...].astype(bf16)) + bu_ref[...]
        gate = jnp.minimum(gate, SWIGLU_LIMIT)
        up = jnp.clip(up, -SWIGLU_LIMIT, SWIGLU_LIMIT)
        act = ((up + 1.0) * gate * jax.nn.sigmoid(SWIGLU_ALPHA * gate)).astype(bf16)
        part = _dot(act, wd_ref[...].astype(bf16))

        @pl.when(j == 0)
        def _():
            acc_scr[...] = part + bd_ref[...]

        @pl.when(j > 0)
        def _():
            acc_scr[...] += part

        @pl.when(j == n_j - 1)
        def _():
            for c in range(ROW_SLABS):
                o_ref[pl.ds(c, bm, stride=ROW_SLABS), :] = acc_scr[:, c * LANES:(c + 1) * LANES]


def _grouped_mlp(x_buf, block_expert, n_used, w_gate_up, b_gate_up, w_down, b_down, layer):
    bm, tn = EXPERT_BM, EXPERT_TN
    n_j = D_FF // tn

    def blk(i, nu):
        return jnp.minimum(i, nu[0] - 1)

    def jj(i, j, nu):
        return jnp.where(i < nu[0], j, n_j - 1)

    def ee(i, be, nu):
        return be[blk(i, nu)]

    grid_spec = pltpu.PrefetchScalarGridSpec(
        num_scalar_prefetch=2,
        grid=(N_BLOCKS, n_j),
        in_specs=[
            pl.BlockSpec((bm * ROW_SLABS, LANES), lambda i, j, be, nu: (blk(i, nu), 0)),
            pl.BlockSpec((None, None, D_MODEL, tn), lambda i, j, be, nu: (layer, ee(i, be, nu), 0, jj(i, j, nu))),
            pl.BlockSpec((None, None, D_MODEL, tn), lambda i, j, be, nu: (layer, ee(i, be, nu), 0, n_j + jj(i, j, nu))),
            pl.BlockSpec((None, None, tn, D_MODEL), lambda i, j, be, nu: (layer, ee(i, be, nu), jj(i, j, nu), 0)),
            pl.BlockSpec((None, None, 1, tn), lambda i, j, be, nu: (layer, ee(i, be, nu), 0, jj(i, j, nu))),
            pl.BlockSpec((None, None, 1, tn), lambda i, j, be, nu: (layer, ee(i, be, nu), 0, n_j + jj(i, j, nu))),
            pl.BlockSpec((None, None, 1, D_MODEL), lambda i, j, be, nu: (layer, ee(i, be, nu), 0, 0)),
        ],
        out_specs=pl.BlockSpec((bm * ROW_SLABS, LANES), lambda i, j, be, nu: (blk(i, nu), 0)),
        scratch_shapes=[pltpu.VMEM((bm, D_MODEL), bf16), pltpu.VMEM((bm, D_MODEL), f32)],
    )
    return pl.pallas_call(
        _gmm_body,
        grid_spec=grid_spec,
        out_shape=jax.ShapeDtypeStruct((ROWS_TOTAL * ROW_SLABS, LANES), f32),
        compiler_params=_params("arbitrary", "arbitrary"),
        name="grouped_mlp",
    )(block_expert, n_used, x_buf, w_gate_up, w_gate_up, w_down,
      b_gate_up.reshape(DEPTH, N_EXPERTS, 1, 2 * D_FF), b_gate_up.reshape(DEPTH, N_EXPERTS, 1, 2 * D_FF),
      b_down.reshape(DEPTH, N_EXPERTS, 1, D_MODEL))


def _combine_body(dest_ref, y_hbm, gates_ref, x_ref, g2_ref, o_ref, rows_scr, sem):
    tm = TM_COMBINE
    n = TOP_K * tm

    def issue(a, c):
        k = lax.shift_right_logical(a, tm.bit_length() - 1)
        t = a & (tm - 1)
        pltpu.make_async_copy(y_hbm.at[dest_ref[0, 0, a]],
                              rows_scr.at[k, pl.ds(pl.multiple_of(t * ROW_SLABS, ROW_SLABS), ROW_SLABS)],
                              sem).start()
        return c

    lax.fori_loop(0, n, issue, 0)

    def wait_one(a, c):
        pltpu.make_async_copy(y_hbm.at[0], rows_scr.at[0, pl.ds(0, ROW_SLABS)], sem).wait()
        return c

    lax.fori_loop(0, n, wait_one, 0)

    gates = gates_ref[...]
    g2 = g2_ref[...]
    for c in range(ROW_SLABS):
        acc = jnp.zeros((tm, LANES), f32)
        for k in range(TOP_K):
            acc = acc + gates[:, k:k + 1] * rows_scr[k, pl.ds(c, tm, stride=ROW_SLABS), :]
        cols = slice(c * LANES, (c + 1) * LANES)
        o_ref[:, cols] = x_ref[:, cols] + g2[:, cols] * acc


def _combine(y_rows, dest_tiles, gates_t, x, mod_tab, layer):
    tm = TM_COMBINE
    return pl.pallas_call(
        _combine_body,
        grid=(N_TOK // tm,),
        in_specs=[
            pl.BlockSpec((1, 1, TOP_K * tm), lambda i: (i, 0, 0), memory_space=pltpu.SMEM),
            pl.BlockSpec(memory_space=pl.ANY),
            pl.BlockSpec((tm, TOP_K), lambda i: (i, 0)),
            pl.BlockSpec((tm, D_MODEL), lambda i: (i, 0)),
            _mod_spec(layer, 5, tm),
        ],
        out_specs=pl.BlockSpec((tm, D_MODEL), lambda i: (i, 0)),
        out_shape=jax.ShapeDtypeStruct((N_TOK, D_MODEL), f32),
        scratch_shapes=[pltpu.VMEM((TOP_K, tm * ROW_SLABS, LANES), f32), pltpu.SemaphoreType.DMA(())],
        compiler_params=_params("arbitrary"),
        name="combine",
    )(dest_tiles, y_rows, gates_t, x, mod_tab)


def _dest_tiles(dest, tm):
    return dest.reshape(TOP_K, N_TOK // tm, tm).transpose(1, 0, 2).reshape(N_TOK // tm, 1, TOP_K * tm)


def _expert_layer(x, gain, mod_tab, wr_t, b_router, upper, w_gate_up, b_gate_up, w_down, b_down, layer):
    h_rows, idx, gates, rank, cnt = _norm_router(x, gain, mod_tab, wr_t, b_router, upper, layer)
    counts = cnt[:, 0].astype(i32)
    padded = (counts + (EXPERT_BM - 1)) // EXPERT_BM * EXPERT_BM
    pad_end = jnp.cumsum(padded)
    pad_start = pad_end - padded
    n_used = (pad_end[-1:] // EXPERT_BM).astype(i32)
    dest = pad_start[idx] + rank
    block_expert = jnp.minimum(
        jnp.searchsorted(pad_end, jnp.arange(N_BLOCKS, dtype=i32) * EXPERT_BM, side='right'),
        N_EXPERTS - 1).astype(i32)
    x_buf = _dispatch(h_rows.reshape(N_TOK, ROW_SLABS, LANES), _dest_tiles(dest, TM_DISPATCH), counts, pad_start)
    y_buf = _grouped_mlp(x_buf.reshape(ROWS_TOTAL * ROW_SLABS, LANES), block_expert, n_used,
                         w_gate_up, b_gate_up, w_down, b_down, layer)
    return _combine(y_buf.reshape(ROWS_TOTAL, ROW_SLABS, LANES), _dest_tiles(dest, TM_COMBINE),
                    gates.T, x, mod_tab, layer)


def _final_norm_body(x_ref, g_ref, o_ref):
    x = x_ref[...]
    o_ref[...] = x * lax.rsqrt(jnp.mean(x * x, axis=-1, keepdims=True) + EPS) * g_ref[...]


def _final_norm(x, gain, row0, n_rows):
    tm = TM_NORM
    blk0 = row0 // tm
    return pl.pallas_call(
        _final_norm_body,
        grid=(n_rows // tm,),
        in_specs=[pl.BlockSpec((tm, D_MODEL), lambda i: (blk0 + i, 0)),
                  pl.BlockSpec((1, D_MODEL), lambda i: (0, 0))],
        out_specs=pl.BlockSpec((tm, D_MODEL), lambda i: (i, 0)),
        out_shape=jax.ShapeDtypeStruct((n_rows, D_MODEL), f32),
        compiler_params=_params("arbitrary"),
        name="final_norm",
    )(x, gain.reshape(1, D_MODEL))


def kernel(x_prompt, x_sample, state_hgrn, state_s5_re, state_s5_im, cache_k, cache_v, c, c_ctx, w_mod, b_mod, norm_gain, w_in, w_out, hg_lb_logits, s5_a_re, s5_a_im, s5_log_dt, s5_b_re, s5_b_im, s5_c_re, s5_c_im, s5_d, w_glu, b_glu, na_rpb, w_router, b_router, w_gate_up, b_gate_up, w_down, b_down, final_norm):
    sm = jax.nn.softmax(hg_lb_logits.astype(f32), axis=0)
    lower_bounds = jnp.cumsum(sm, axis=0) - sm[0:1]

    c_all = jnp.concatenate([c_ctx[None], c, jnp.zeros((MOD_ROWS - N_GROUPS, D_MODEL), f32)], axis=0)
    mod_tab = _modulation(c_all, w_mod, b_mod).reshape(DEPTH * MOD_ROWS * 6, 1, D_MODEL)

    w_in_bf = w_in.astype(bf16)
    w_out_bf = w_out.astype(bf16)
    w_glu_bf = w_glu.astype(bf16)
    wr_hi = w_router.astype(bf16)
    wr_lo = (w_router - wr_hi.astype(f32)).astype(bf16)
    wr_t = jnp.concatenate([wr_hi, wr_lo], axis=-1).transpose(0, 2, 1)
    upper = (jnp.arange(TM_NORM)[:, None] < jnp.arange(TM_NORM)[None, :]).astype(bf16)

    x = jnp.concatenate([x_prompt.reshape(N_CTX, D_MODEL), x_sample.reshape(N_LAT, D_MODEL)], axis=0)
    two_p = 2 * S5_STATE
    hg_states, s5_states, k_caches, v_caches = [], [], [], []
    s5_col = 5 * HG_WIDTH
    for l in range(DEPTH):
        tables = _s5_tables(s5_a_re[l], s5_a_im[l], s5_log_dt[l], s5_b_re[l], s5_b_im[l],
                            s5_c_re[l], s5_c_im[l], s5_d[l])
        bias_tab = _na_bias_table(na_rpb[l])

        h = _norm_mod(x, norm_gain[l, 0], mod_tab, l)
        z = _in_proj(h, w_in_bf, l)

        hg_ctx, s_fin = _hgrn(z, lower_bounds[l], None, l, latent=False)
        hg_lat, _ = _hgrn(z, lower_bounds[l], state_hgrn, l, latent=True)
        hg_states.append(s_fin)

        z_s5 = z[:, s5_col:s5_col + S5_WIDTH]
        h0_ctx = jnp.zeros((S5_GROUPS, 2, BATCH, two_p), f32)
        h0_lat = jnp.concatenate([state_s5_re[:, l], state_s5_im[:, l]], axis=-1).transpose(2, 1, 0, 3)
        y_ctx, hfin = _s5_scan(_s5_rows(z_s5[:N_CTX], BATCH, SEQ), h0_ctx, tables, SEQ // S5_CHUNK, BATCH, "s5_ctx")
        y_lat, _ = _s5_scan(_s5_rows(z_s5[N_CTX:], DEC_BATCH, DEC_SEQ), h0_lat, tables,
                            DEC_SEQ // S5_CHUNK, DEC_BATCH, "s5_lat")
        s5_states.append(hfin.transpose(2, 1, 0, 3))
        y_s5 = jnp.concatenate([_s5_unrows(y_ctx, BATCH, SEQ), _s5_unrows(y_lat, DEC_BATCH, DEC_SEQ)], axis=0)
        s5_out = _s5_glu(y_s5, w_glu_bf, b_glu, l)

        na_ctx, k_ctx, v_ctx = _ctx_attention(z)
        na_lat = _neighbourhood_attention(z, cache_k, cache_v, bias_tab, l)
        k_caches.append(k_ctx)
        v_caches.append(v_ctx)

        x = _out_proj(jnp.concatenate([hg_ctx, hg_lat], axis=0), s5_out,
                      jnp.concatenate([na_ctx, na_lat], axis=0), w_out_bf, x, mod_tab, l)
        x = _expert_layer(x, norm_gain[l, 1], mod_tab, wr_t, b_router, upper,
                          w_gate_up, b_gate_up, w_down, b_down, l)

    y_prompt = _final_norm(x, final_norm, 0, N_CTX).reshape(BATCH, SEQ, D_MODEL)
    y_sample = _final_norm(x, final_norm, N_CTX, N_LAT).reshape(DEC_BATCH, DEC_SEQ, D_MODEL)
    new_state_hgrn = jnp.stack(hg_states, axis=1)
    s5_all = jnp.stack(s5_states, axis=1)
    new_cache_k = jnp.stack(k_caches, axis=1)
    new_cache_v = jnp.stack(v_caches, axis=1)
    return (y_prompt, y_sample, new_state_hgrn, s5_all[..., :S5_STATE], s5_all[..., S5_STATE:],
            new_cache_k, new_cache_v)
```
